```python
import math
import jax, jax.numpy as jnp
from jax import lax
import numpy as np

D_MODEL = 1024
BATCH = 4
SEQ = 8192
DEPTH = 4

MIX_WIDTH = D_MODEL
HEAD_DIM = 64
CONV_WIDTH = D_MODEL // 4
NSA_HEADS = 8
NSA_KV_GROUPS = 2
NSA_WIDTH = NSA_HEADS * HEAD_DIM
KV_WIDTH = NSA_KV_GROUPS * HEAD_DIM
MEM_HEADS = 4
MEM_WIDTH = MEM_HEADS * HEAD_DIM
N_MEM = 256
CONV_K = 3
CMP_BLOCK = 32
CMP_STRIDE = 16
CMP_HIDDEN = 256
SEL_BLOCK = 64
SEL_TOPK = 16
WINDOW = 512
Q_BLOCK = 128
N_BRANCH = 3
RMS_EPS = 1e-6
FORCE_SCORE = 1e4
IN_SPLITS = (CONV_WIDTH, CONV_WIDTH, CONV_WIDTH, CONV_WIDTH,
             NSA_WIDTH, KV_WIDTH, KV_WIDTH, KV_WIDTH, KV_WIDTH, KV_WIDTH, KV_WIDTH,
             N_BRANCH * NSA_HEADS, NSA_WIDTH,
             MEM_WIDTH, MEM_WIDTH)
IN_WIDTH = sum(IN_SPLITS)

kernel_name = "hymba_conv_nsa_memory_trunk"


def rmsnorm(x, g):
    xf = x.astype(jnp.float32)
    y = xf * lax.rsqrt(jnp.mean(xf * xf, axis=-1, keepdims=True) + RMS_EPS)
    return (y * g.astype(jnp.float32)).astype(x.dtype)


def alibi_slopes(n):
    return jnp.exp2(-8.0 * jnp.arange(1, n + 1, dtype=jnp.float32) / n)


def masked_softmax(s, mask):
    s = jnp.where(mask, s.astype(jnp.float32), -jnp.inf)
    m = jnp.max(s, axis=-1, keepdims=True)
    m = jnp.where(jnp.isfinite(m), m, 0.0)
    p = jnp.exp(s - m)
    return p / jnp.maximum(jnp.sum(p, axis=-1, keepdims=True), 1e-30)


def short_conv(b, c, h, w, bias):
    T = h.shape[1]
    up = jnp.pad(c * h, ((0, 0), (CONV_K - 1, 0), (0, 0)))
    y = bias + up[:, 0:T] * w[0]
    for k in range(1, CONV_K):
        y = y + up[:, k:k + T] * w[k]
    return b * y


def compress_kv(k, pos, w1, w2):
    B, G, T, HD = k.shape
    ch = k.reshape(B, G, T // CMP_STRIDE, CMP_STRIDE, HD)
    blocks = jnp.concatenate([ch[:, :, :-1], ch[:, :, 1:]], axis=3) + pos
    flat = blocks.reshape(B, G, blocks.shape[2], CMP_BLOCK * HD)
    return jax.nn.gelu(flat @ w1) @ w2


def nsa_attention(q, k_cmp, v_cmp, k_sel, v_sel, k_win, v_win, gate_logits,
                  pos_k, w1_k, w2_k, pos_v, w1_v, w2_v):
    B, T, _ = q.shape
    G, R, HD = NSA_KV_GROUPS, NSA_HEADS // NSA_KV_GROUPS, HEAD_DIM
    n_cmp = T // CMP_STRIDE - 1
    n_sel = T // SEL_BLOCK
    top_k = min(SEL_TOPK, n_sel)
    n_qb = T // Q_BLOCK

    def to_heads(t, n):
        return t.reshape(B, T, n, HD).transpose(0, 2, 1, 3)

    qh = (to_heads(q, NSA_HEADS) * HD ** -0.5).reshape(B, G, R, T, HD)
    kc = compress_kv(to_heads(k_cmp, G), pos_k, w1_k, w2_k)
    vc = compress_kv(to_heads(v_cmp, G), pos_v, w1_v, w2_v)
    ks_blk = to_heads(k_sel, G).reshape(B, G, n_sel, SEL_BLOCK, HD)
    vs_blk = to_heads(v_sel, G).reshape(B, G, n_sel, SEL_BLOCK, HD)
    wpad = ((0, 0), (0, 0), (WINDOW, 0), (0, 0))
    kw = jnp.pad(to_heads(k_win, G), wpad)
    vw = jnp.pad(to_heads(v_win, G), wpad)
    gates = jax.nn.sigmoid(gate_logits.astype(jnp.float32))
    gates = gates.reshape(B, T, N_BRANCH, G, R).transpose(2, 0, 3, 4, 1)[..., None]
    slopes = alibi_slopes(NSA_HEADS).reshape(1, G, R, 1, 1)
    cmp_end = jnp.arange(n_cmp) * CMP_STRIDE + (CMP_BLOCK - 1)
    blk = jnp.arange(n_sel)
    b_ix = jnp.arange(B)[:, None, None, None]
    g_ix = jnp.arange(G)[None, :, None, None]

    def step(qb):
        q0 = qb * Q_BLOCK
        qi = lax.dynamic_slice_in_dim(qh, q0, Q_BLOCK, axis=3)
        gi = lax.dynamic_slice_in_dim(gates, q0, Q_BLOCK, axis=4)
        t = q0 + jnp.arange(Q_BLOCK)
        d_c = (t[:, None] - cmp_end[None, :]).astype(jnp.float32)
        s_c = jnp.einsum('bgrqd,bgnd->bgrqn', qi, kc) - slopes * d_c
        p_c = masked_softmax(s_c, d_c >= 0)
        o_cmp = jnp.einsum('bgrqn,bgnd->bgrqd', p_c, vc)
        imp = jnp.sum(p_c, axis=2)
        chunk = (jnp.pad(imp, ((0, 0), (0, 0), (0, 0), (0, 1)))
                 + jnp.pad(imp, ((0, 0), (0, 0), (0, 0), (1, 0))))
        imp_sel = chunk.reshape(B, G, Q_BLOCK, n_sel, SEL_BLOCK // CMP_STRIDE).sum(-1)
        cur = t // SEL_BLOCK
        forced = ((blk[None, :] == 0) | (blk[None, :] == cur[:, None])
                  | (blk[None, :] == cur[:, None] - 1))
        future = blk[None, :] > cur[:, None]
        imp_sel = jnp.where(forced, FORCE_SCORE, jnp.where(future, -1.0, imp_sel))
        _, idx = lax.top_k(imp_sel, top_k)
        k_g = ks_blk[b_ix, g_ix, idx]
        v_g = vs_blk[b_ix, g_ix, idx]
        spos = idx[..., None] * SEL_BLOCK + jnp.arange(SEL_BLOCK)
        d_s = (t[:, None, None] - spos).astype(jnp.float32)[:, :, None]
        s_s = jnp.einsum('bgrqd,bgqnsd->bgrqns', qi, k_g) - slopes[..., None] * d_s
        m_s = jnp.broadcast_to(d_s >= 0, s_s.shape)
        shp = s_s.shape
        p_s = masked_softmax(s_s.reshape(shp[:4] + (-1,)), m_s.reshape(shp[:4] + (-1,))).reshape(shp)
        o_sel = jnp.einsum('bgrqns,bgqnsd->bgrqd', p_s, v_g)
        kwi = lax.dynamic_slice_in_dim(kw, q0, Q_BLOCK + WINDOW, axis=2)
        vwi = lax.dynamic_slice_in_dim(vw, q0, Q_BLOCK + WINDOW, axis=2)
        wpos = q0 - WINDOW + jnp.arange(Q_BLOCK + WINDOW)
        d_w = t[:, None] - wpos[None, :]
        m_w = (d_w >= 0) & (d_w < WINDOW) & (wpos[None, :] >= 0)
        s_w = jnp.einsum('bgrqd,bgsd->bgrqs', qi, kwi) - slopes * d_w.astype(jnp.float32)
        o_win = jnp.einsum('bgrqs,bgsd->bgrqd', masked_softmax(s_w, m_w), vwi)
        return gi[0] * o_cmp + gi[1] * o_sel + gi[2] * o_win

    out = lax.map(step, jnp.arange(n_qb))
    return out.transpose(1, 0, 4, 2, 3, 5).reshape(B, T, NSA_WIDTH).astype(q.dtype)


def memory_attention(q, mem_n, w_mem_kv):
    B, T, _ = q.shape
    kv = mem_n @ w_mem_kv
    k, v = jnp.split(kv, 2, axis=-1)
    qh = q.reshape(B, T, MEM_HEADS, HEAD_DIM) * HEAD_DIM ** -0.5
    kh = k.reshape(B, -1, MEM_HEADS, HEAD_DIM)
    vh = v.reshape(B, -1, MEM_HEADS, HEAD_DIM)
    p = jax.nn.softmax(jnp.einsum('bthd,bmhd->bhtm', qh, kh).astype(jnp.float32), axis=-1)
    o = jnp.einsum('bhtm,bmhd->bthd', p, vh)
    return o.reshape(B, T, MEM_WIDTH).astype(q.dtype)


def setup_inputs(seed: int = 0) -> dict:
    key = jax.random.key(seed)
    ks = jax.random.split(key, 20)
    L, D = DEPTH, D_MODEL
    f32 = jnp.float32

    def nrm(k, shape, fan_in):
        return jax.random.normal(k, shape, f32) * fan_in ** -0.5

    def gain(k):
        return 1.0 + 0.02 * jax.random.normal(k, (L, D), f32)

    return {
        "x": jax.random.normal(ks[0], (BATCH, SEQ, D), f32),
        "mem": jax.random.normal(ks[1], (BATCH, N_MEM, D), f32),
        "pre_norm_g": gain(ks[2]),
        "post_norm_g": gain(ks[3]),
        "mem_norm_g": gain(ks[4]),
        "w_in": nrm(ks[5], (L, D, IN_WIDTH), D),
        "b_gate": 0.01 * jax.random.normal(ks[6], (L, N_BRANCH * NSA_HEADS), f32),
        "conv_w": nrm(ks[7], (L, CONV_K, CONV_WIDTH), CONV_K),
        "conv_b": 0.01 * jax.random.normal(ks[8], (L, CONV_WIDTH), f32),
        "cmp_pos_k": 0.02 * jax.random.normal(ks[9], (L, CMP_BLOCK, HEAD_DIM), f32),
        "cmp_w1_k": nrm(ks[10], (L, CMP_BLOCK * HEAD_DIM, CMP_HIDDEN), CMP_BLOCK * HEAD_DIM),
        "cmp_w2_k": nrm(ks[11], (L, CMP_HIDDEN, HEAD_DIM), CMP_HIDDEN),
        "cmp_pos_v": 0.02 * jax.random.normal(ks[12], (L, CMP_BLOCK, HEAD_DIM), f32),
        "cmp_w1_v": nrm(ks[13], (L, CMP_BLOCK * HEAD_DIM, CMP_HIDDEN), CMP_BLOCK * HEAD_DIM),
        "cmp_w2_v": nrm(ks[14], (L, CMP_HIDDEN, HEAD_DIM), CMP_HIDDEN),
        "w_mem_kv": nrm(ks[15], (L, D, 2 * MEM_WIDTH), D),
        "w_out": nrm(ks[16], (L, MIX_WIDTH, D), MIX_WIDTH),
    }


def reference(x, mem, pre_norm_g, post_norm_g, mem_norm_g, w_in, b_gate, conv_w, conv_b,
              cmp_pos_k, cmp_w1_k, cmp_w2_k, cmp_pos_v, cmp_w1_v, cmp_w2_v, w_mem_kv, w_out):
    offs = np.cumsum(IN_SPLITS)[:-1].tolist()
    for l in range(DEPTH):
        h = rmsnorm(x, pre_norm_g[l])
        proj = h @ w_in[l]
        (c_b, c_c, c_h, c_gate, n_q, n_kc, n_vc, n_ks, n_vs, n_kw, n_vw,
         n_glog, n_gate, m_q, m_gate) = jnp.split(proj, offs, axis=-1)
        y_conv = short_conv(c_b, c_c, c_h, conv_w[l], conv_b[l]) * jax.nn.silu(c_gate)
        y_nsa = nsa_attention(n_q, n_kc, n_vc, n_ks, n_vs, n_kw, n_vw, n_glog + b_gate[l],
                              cmp_pos_k[l], cmp_w1_k[l], cmp_w2_k[l],
                              cmp_pos_v[l], cmp_w1_v[l], cmp_w2_v[l]) * jax.nn.silu(n_gate)
        y_mem = memory_attention(m_q, rmsnorm(mem, mem_norm_g[l]), w_mem_kv[l]) * jax.nn.silu(m_gate)
        y = jnp.concatenate([y_conv, y_nsa, y_mem], axis=-1) @ w_out[l]
        x = x + rmsnorm(y, post_norm_g[l])
    return x
```

```python
import functools

import jax
import jax.numpy as jnp
from jax import lax
from jax.experimental import pallas as pl
from jax.experimental.pallas import tpu as pltpu

F32 = jnp.float32
BF16 = jnp.bfloat16

HEAD_DIM = 64
CONV_WIDTH = 256
NSA_HEADS = 8
NSA_GROUPS = 2
HEADS_PER_GROUP = NSA_HEADS // NSA_GROUPS
NSA_WIDTH = NSA_HEADS * HEAD_DIM
KV_WIDTH = NSA_GROUPS * HEAD_DIM
MEM_HEADS = 4
MEM_WIDTH = MEM_HEADS * HEAD_DIM
CMP_STRIDE = 16
CMP_BLOCK = 32
SEL_BLOCK = 64
SEL_TOPK = 16
WINDOW = 512
Q_BLOCK = 128
N_BRANCH = 3
RMS_EPS = 1e-6
FORCE_SCORE = 1e4
Q_SCALE = HEAD_DIM ** -0.5
MASK_VALUE = -1e30

LANES = 128
MAX_SEL_BLOCKS = LANES

OFF_CONV = 0
OFF_Q = 1024
OFF_NGATE = 1536
OFF_MQ = 2048
OFF_MGATE = 2304
OFF_KC = 2560
OFF_VC = 2688
OFF_KS = 2816
OFF_VS = 2944
OFF_KW = 3072
OFF_VW = 3200
OFF_GLOG = 3328
PROJ_WIDTH = 3456
N_GLOG = N_BRANCH * NSA_HEADS

ROW_TILE = 512
SEL_CHUNK = 512
VMEM_LIMIT = 48 * 1024 * 1024

_NT = (((1,), (1,)), ((), ()))


def _sigmoid(x):
    return 1.0 / (1.0 + jnp.exp(-x))


def _silu(x):
    return x * _sigmoid(x)


def _gelu_tanh(x):
    return x * (0.5 * (1.0 + jnp.tanh(0.7978845608028654 * (x + 0.044715 * (x * x * x)))))


def _masked_softmax(s, valid):
    s = jnp.where(valid, s, -jnp.inf)
    m = jnp.max(s, axis=-1, keepdims=True)
    m = jnp.where(jnp.isfinite(m), m, 0.0)
    p = jnp.exp(s - m)
    return p / jnp.maximum(jnp.sum(p, axis=-1, keepdims=True), 1e-30)


def _inproj_kernel(x_ref, g_ref, w_ref, o_ref):
    x = x_ref[...]
    ms = jnp.mean(x * x, axis=-1, keepdims=True)
    h = (x * lax.rsqrt(ms + RMS_EPS) * g_ref[...]).astype(BF16)
    for j0 in range(0, PROJ_WIDTH, 768):
        j1 = min(j0 + 768, PROJ_WIDTH)
        o_ref[:, j0:j1] = jnp.dot(h, w_ref[:, j0:j1], preferred_element_type=F32).astype(BF16)


def _inproj(x2d, g, w):
    n, d = x2d.shape
    return pl.pallas_call(
        _inproj_kernel,
        out_shape=jax.ShapeDtypeStruct((n, PROJ_WIDTH), BF16),
        grid=(n // ROW_TILE,),
        in_specs=[
            pl.BlockSpec((ROW_TILE, d), lambda i: (i, 0)),
            pl.BlockSpec((1, d), lambda i: (0, 0)),
            pl.BlockSpec((d, PROJ_WIDTH), lambda i: (0, 0)),
        ],
        out_specs=pl.BlockSpec((ROW_TILE, PROJ_WIDTH), lambda i: (i, 0)),
        compiler_params=pltpu.CompilerParams(
            dimension_semantics=("parallel",), vmem_limit_bytes=VMEM_LIMIT),
        name="inproj",
    )(x2d, g, w)


def _compress_kernel(x_ref, w1_ref, pos_ref, w2_ref, o_ref):
    half = CMP_STRIDE * HEAD_DIM
    nc = x_ref.shape[1]
    outs = []
    for g in range(NSA_GROUPS):
        x = x_ref[g].astype(F32)
        xa = (x + pos_ref[:, :half]).astype(BF16)
        xb = (x + pos_ref[:, half:]).astype(BF16)
        a = jnp.dot(xa, w1_ref[:half, :], preferred_element_type=F32)
        b = jnp.dot(xb, w1_ref[half:, :], preferred_element_type=F32)
        hid = _gelu_tanh(a + pltpu.roll(b, nc - 1, axis=0))
        outs.append(jnp.dot(hid.astype(BF16), w2_ref[...], preferred_element_type=F32))
    o_ref[...] = jnp.concatenate(outs, axis=1).astype(BF16)


def _compress(xg, w1, pos, w2):
    _, b, _, nc, half = xg.shape
    hidden = w1.shape[-1]
    return pl.pallas_call(
        _compress_kernel,
        out_shape=jax.ShapeDtypeStruct((2, b, nc, KV_WIDTH), BF16),
        grid=(2, b),
        in_specs=[
            pl.BlockSpec((None, None, NSA_GROUPS, nc, half), lambda a, i: (a, i, 0, 0, 0)),
            pl.BlockSpec((None, 2 * half, hidden), lambda a, i: (a, 0, 0)),
            pl.BlockSpec((None, 1, 2 * half), lambda a, i: (a, 0, 0)),
            pl.BlockSpec((None, hidden, HEAD_DIM), lambda a, i: (a, 0, 0)),
        ],
        out_specs=pl.BlockSpec((None, None, nc, KV_WIDTH), lambda a, i: (a, i, 0, 0)),
        compiler_params=pltpu.CompilerParams(
            dimension_semantics=("parallel", "parallel"), vmem_limit_bytes=VMEM_LIMIT),
        name="compress_kv",
    )(xg, w1, pos, w2)


def _memkv_kernel(mem_ref, g_ref, w_ref, o_ref):
    x = mem_ref[...]
    ms = jnp.mean(x * x, axis=-1, keepdims=True)
    h = (x * lax.rsqrt(ms + RMS_EPS) * g_ref[...]).astype(BF16)
    o_ref[...] = jnp.dot(h, w_ref[...], preferred_element_type=F32).astype(BF16)


def _memkv(mem, g, w):
    b, m, d = mem.shape
    n_layers = w.shape[0]
    return pl.pallas_call(
        _memkv_kernel,
        out_shape=jax.ShapeDtypeStruct((n_layers, b, m, 2 * MEM_WIDTH), BF16),
        grid=(n_layers, b),
        in_specs=[
            pl.BlockSpec((None, m, d), lambda l, i: (i, 0, 0)),
            pl.BlockSpec((None, 1, d), lambda l, i: (l, 0, 0)),
            pl.BlockSpec((None, d, 2 * MEM_WIDTH), lambda l, i: (l, 0, 0)),
        ],
        out_specs=pl.BlockSpec((None, None, m, 2 * MEM_WIDTH), lambda l, i: (l, i, 0, 0)),
        compiler_params=pltpu.CompilerParams(
            dimension_semantics=("parallel", "parallel"), vmem_limit_bytes=VMEM_LIMIT),
        name="mem_kv",
    )(mem, g, w)


def _nsa_kernel(q_ref, glog_ref, ngate_ref, kc_ref, vc_ref, ks_ref, vs_ref, kw_ref, vw_ref,
                et_ref, mimp_ref, bgate_ref, o_ref, m_scr, l_scr, acc_scr):
    g = pl.program_id(1)
    i = pl.program_id(2)
    q0 = i * Q_BLOCK
    rows = HEADS_PER_GROUP * Q_BLOCK
    n_cmp_pad = kc_ref.shape[0]

    lane = lax.broadcasted_iota(jnp.int32, (Q_BLOCK, LANES), 1)
    slot = lane >> 6
    in_slot = slot == g

    qf = q_ref[...].astype(F32)
    parts = []
    for r in range(HEADS_PER_GROUP):
        pair = qf[:, (r // 2) * LANES:(r // 2 + 1) * LANES]
        placed = jnp.where(g == (r % 2), pair, pltpu.roll(pair, HEAD_DIM, axis=1))
        parts.append(jnp.where(in_slot, placed * Q_SCALE, 0.0))
    qa = jnp.concatenate(parts, axis=0).astype(BF16)

    row = lax.broadcasted_iota(jnp.int32, (rows, 1), 0)
    tq = q0 + (row & (Q_BLOCK - 1))
    head = HEADS_PER_GROUP * g + (row >> 7)
    slope = lax.bitcast_convert_type((126 - head) << 23, F32)

    s = lax.dot_general(qa, kc_ref[...], _NT, preferred_element_type=F32)
    n_idx = lax.broadcasted_iota(jnp.int32, (1, n_cmp_pad), 1)
    d_c = tq - (n_idx * CMP_STRIDE + (CMP_BLOCK - 1))
    p_c = _masked_softmax(s - slope * d_c.astype(F32), d_c >= 0)
    o_cmp = jnp.dot(p_c.astype(BF16), vc_ref[...], preferred_element_type=F32)

    imp = p_c[0:Q_BLOCK]
    for r in range(1, HEADS_PER_GROUP):
        imp = imp + p_c[r * Q_BLOCK:(r + 1) * Q_BLOCK]
    imp_sel = jnp.dot(imp, mimp_ref[...], preferred_element_type=F32,
                      precision=lax.Precision.HIGHEST)
    blk = lane
    cur = (q0 + lax.broadcasted_iota(jnp.int32, (Q_BLOCK, 1), 0)) >> 6
    forced = (blk == 0) | (blk == cur) | (blk == cur - 1)
    val = jnp.where(forced, FORCE_SCORE, jnp.where(blk > cur, -1.0, imp_sel))
    blk_f = blk.astype(F32)
    sel = jnp.zeros((Q_BLOCK, LANES), F32)
    for _ in range(SEL_TOPK):
        mx = jnp.max(val, axis=-1, keepdims=True)
        first = jnp.min(jnp.where(val == mx, blk_f, float(LANES)), axis=-1, keepdims=True)
        hit = blk_f == first
        sel = jnp.where(hit, 1.0, sel)
        val = jnp.where(hit, -jnp.inf, val)

    sel4 = jnp.concatenate([sel] * HEADS_PER_GROUP, axis=0)
    blk4 = jnp.concatenate([blk_f] * HEADS_PER_GROUP, axis=0)
    bias = jnp.where(sel4 > 0.0, slope * float(SEL_BLOCK) * blk4, MASK_VALUE).astype(BF16)
    qfull = jnp.concatenate([qa, bias], axis=1)
    koff = lax.broadcasted_iota(jnp.int32, (1, SEL_CHUNK), 1)
    off_bias = slope * (koff & (SEL_BLOCK - 1)).astype(F32)

    m_scr[...] = jnp.full(m_scr.shape, MASK_VALUE, F32)
    l_scr[...] = jnp.zeros(l_scr.shape, F32)
    acc_scr[...] = jnp.zeros(acc_scr.shape, F32)

    def flash_step(c, causal):
        k0 = pl.multiple_of(c * SEL_CHUNK, SEL_CHUNK)
        kk = jnp.concatenate([ks_ref[pl.ds(k0, SEL_CHUNK), :], et_ref[pl.ds(k0, SEL_CHUNK), :]], axis=1)
        sc = lax.dot_general(qfull, kk, _NT, preferred_element_type=F32) + off_bias
        if causal:
            sc = jnp.where(k0 + koff <= tq, sc, MASK_VALUE)
        m_old = m_scr[...]
        m_new = jnp.maximum(m_old, jnp.max(sc, axis=-1, keepdims=True))
        alpha = jnp.exp(m_old - m_new)
        p = jnp.exp(sc - m_new)
        l_scr[...] = alpha * l_scr[...] + jnp.sum(p, axis=-1, keepdims=True)
        acc_scr[...] = alpha * acc_scr[...] + jnp.dot(
            p.astype(BF16), vs_ref[pl.ds(k0, SEL_CHUNK), :], preferred_element_type=F32)
        m_scr[...] = m_new

    c_diag = q0 // SEL_CHUNK

    def body(c, carry):
        flash_step(c, causal=False)
        return carry

    lax.fori_loop(0, c_diag, body, 0)
    flash_step(c_diag, causal=True)
    o_sel = acc_scr[...] / l_scr[...]

    w0 = pl.multiple_of(jnp.maximum(q0 - WINDOW, 0), Q_BLOCK)
    n_win = WINDOW + Q_BLOCK
    s = lax.dot_general(qa, kw_ref[pl.ds(w0, n_win), :], _NT, preferred_element_type=F32)
    d_w = tq - (w0 + lax.broadcasted_iota(jnp.int32, (1, n_win), 1))
    p_w = _masked_softmax(s - slope * d_w.astype(F32), (d_w >= 0) & (d_w < WINDOW))
    o_win = jnp.dot(p_w.astype(BF16), vw_ref[pl.ds(w0, n_win), :], preferred_element_type=F32)

    gates = _sigmoid(glog_ref[...].astype(F32) + bgate_ref[...])

    def gate_col(branch, r):
        idx = branch * NSA_HEADS + HEADS_PER_GROUP * g + r
        return jnp.sum(jnp.where(lane == idx, gates, 0.0), axis=-1, keepdims=True)

    heads = []
    for r in range(HEADS_PER_GROUP):
        sl = slice(r * Q_BLOCK, (r + 1) * Q_BLOCK)
        heads.append(gate_col(0, r) * o_cmp[sl] + gate_col(1, r) * o_sel[sl] + gate_col(2, r) * o_win[sl])
    pairs = []
    for j in range(HEADS_PER_GROUP // 2):
        a, b = heads[2 * j], heads[2 * j + 1]
        left = jnp.where(g == 0, a, pltpu.roll(a, HEAD_DIM, axis=1))
        right = jnp.where(g == 0, pltpu.roll(b, HEAD_DIM, axis=1), b)
        pairs.append(jnp.where(slot == 0, left, right))
    y = jnp.concatenate(pairs, axis=1)
    o_ref[...] = (y * _silu(ngate_ref[...].astype(F32))).astype(BF16)


def _nsa(proj, kvc, et, mimp, bgate):
    b, t, _ = proj.shape
    nc = kvc.shape[2]
    rows = HEADS_PER_GROUP * Q_BLOCK
    full = lambda col: pl.BlockSpec((None, t, LANES), lambda bi, g, i: (bi, 0, col // LANES))
    return pl.pallas_call(
        _nsa_kernel,
        out_shape=jax.ShapeDtypeStruct((b, t, NSA_WIDTH), BF16),
        grid=(b, NSA_GROUPS, t // Q_BLOCK),
        in_specs=[
            pl.BlockSpec((None, Q_BLOCK, 2 * LANES), lambda bi, g, i: (bi, i, OFF_Q // (2 * LANES) + g)),
            pl.BlockSpec((None, Q_BLOCK, LANES), lambda bi, g, i: (bi, i, OFF_GLOG // LANES)),
            pl.BlockSpec((None, Q_BLOCK, 2 * LANES), lambda bi, g, i: (bi, i, OFF_NGATE // (2 * LANES) + g)),
            pl.BlockSpec((None, None, nc, KV_WIDTH), lambda bi, g, i: (0, bi, 0, 0)),
            pl.BlockSpec((None, None, nc, KV_WIDTH), lambda bi, g, i: (1, bi, 0, 0)),
            full(OFF_KS), full(OFF_VS), full(OFF_KW), full(OFF_VW),
            pl.BlockSpec((t, LANES), lambda bi, g, i: (0, 0)),
            pl.BlockSpec((nc, LANES), lambda bi, g, i: (0, 0)),
            pl.BlockSpec((1, LANES), lambda bi, g, i: (0, 0)),
        ],
        out_specs=pl.BlockSpec((None, Q_BLOCK, 2 * LANES), lambda bi, g, i: (bi, i, g)),
        scratch_shapes=[
            pltpu.VMEM((rows, 1), F32),
            pltpu.VMEM((rows, 1), F32),
            pltpu.VMEM((rows, LANES), F32),
        ],
        compiler_params=pltpu.CompilerParams(
            dimension_semantics=("parallel", "arbitrary", "arbitrary"), vmem_limit_bytes=VMEM_LIMIT),
        name="nsa_attention",
    )(proj, proj, proj, kvc, kvc, proj, proj, proj, proj, et, mimp, bgate)


def _mix_out_kernel(conv_ref, halo_ref, mem_ref, ynsa_ref, memkv_ref, convw_ref, convb_ref,
                    wout_ref, g_ref, x_ref, o_ref):
    i = pl.program_id(1)
    tm = conv_ref.shape[0]
    cw = CONV_WIDTH

    pc = conv_ref[...].astype(F32)
    u = pc[:, cw:2 * cw] * pc[:, 2 * cw:3 * cw]
    hal = halo_ref[...].astype(F32)
    hu = jnp.where(i > 0, hal[:, cw:2 * cw] * hal[:, 2 * cw:3 * cw], 0.0)
    row = lax.broadcasted_iota(jnp.int32, (tm, 1), 0)
    u1 = jnp.where(row == 0, hu[7:8], pltpu.roll(u, 1, axis=0))
    u2 = jnp.where(row == 0, hu[6:7], jnp.where(row == 1, hu[7:8], pltpu.roll(u, 2, axis=0)))
    w = convw_ref[...]
    y_conv = pc[:, 0:cw] * (convb_ref[...] + w[0:1] * u2 + w[1:2] * u1 + w[2:3] * u) * _silu(pc[:, 3 * cw:4 * cw])

    pm = mem_ref[...].astype(F32)
    kv = memkv_ref[...]
    slot = lax.broadcasted_iota(jnp.int32, (tm, LANES), 1) >> 6
    mem_pairs = []
    for j in range(MEM_HEADS // 2):
        qpair = pm[:, j * LANES:(j + 1) * LANES] * Q_SCALE
        kpair = kv[:, j * LANES:(j + 1) * LANES]
        vpair = kv[:, MEM_WIDTH + j * LANES:MEM_WIDTH + (j + 1) * LANES]
        outs = []
        for sl in range(2):
            qh = jnp.where(slot == sl, qpair, 0.0).astype(BF16)
            s = lax.dot_general(qh, kpair, _NT, preferred_element_type=F32)
            p = jnp.exp(s - jnp.max(s, axis=-1, keepdims=True))
            p = p / jnp.sum(p, axis=-1, keepdims=True)
            outs.append(jnp.dot(p.astype(BF16), vpair, preferred_element_type=F32))
        mem_pairs.append(jnp.where(slot == 0, outs[0], outs[1]))
    y_mem = jnp.concatenate(mem_pairs, axis=1) * _silu(pm[:, MEM_WIDTH:2 * MEM_WIDTH])

    y = jnp.concatenate([y_conv.astype(BF16), ynsa_ref[...], y_mem.astype(BF16)], axis=1)
    z = jnp.dot(y, wout_ref[...], preferred_element_type=F32)
    ms = jnp.mean(z * z, axis=-1, keepdims=True)
    o_ref[...] = x_ref[...] + z * lax.rsqrt(ms + RMS_EPS) * g_ref[...]


def _mix_out(proj, ynsa, memkv, convw, convb, wout, g, x):
    b, t, d = x.shape
    tm = ROW_TILE
    m = memkv.shape[1]
    return pl.pallas_call(
        _mix_out_kernel,
        out_shape=jax.ShapeDtypeStruct((b, t, d), F32),
        grid=(b, t // tm),
        in_specs=[
            pl.BlockSpec((None, tm, 4 * CONV_WIDTH), lambda bi, i: (bi, i, 0)),
            pl.BlockSpec((None, 8, 4 * CONV_WIDTH), lambda bi, i: (bi, jnp.maximum(i * (tm // 8) - 1, 0), 0)),
            pl.BlockSpec((None, tm, 2 * MEM_WIDTH), lambda bi, i: (bi, i, OFF_MQ // (2 * MEM_WIDTH))),
            pl.BlockSpec((None, tm, NSA_WIDTH), lambda bi, i: (bi, i, 0)),
            pl.BlockSpec((None, m, 2 * MEM_WIDTH), lambda bi, i: (bi, 0, 0)),
            pl.BlockSpec((3, CONV_WIDTH), lambda bi, i: (0, 0)),
            pl.BlockSpec((1, CONV_WIDTH), lambda bi, i: (0, 0)),
            pl.BlockSpec((d, d), lambda bi, i: (0, 0)),
            pl.BlockSpec((1, d), lambda bi, i: (0, 0)),
            pl.BlockSpec((None, tm, d), lambda bi, i: (bi, i, 0)),
        ],
        out_specs=pl.BlockSpec((None, tm, d), lambda bi, i: (bi, i, 0)),
        compiler_params=pltpu.CompilerParams(
            dimension_semantics=("parallel", "parallel"), vmem_limit_bytes=VMEM_LIMIT),
        name="mix_out",
    )(proj, proj, proj, ynsa, memkv, convw, convb, wout, g, x)


def _permute_w_in(w_in):
    n_layers, d, _ = w_in.shape
    c_q_end = 4 * CONV_WIDTH + NSA_WIDTH
    c_kv_end = c_q_end + 6 * KV_WIDTH
    c_glog_end = c_kv_end + N_GLOG
    c_ngate_end = c_glog_end + NSA_WIDTH
    return jnp.concatenate([
        w_in[:, :, :c_q_end],
        w_in[:, :, c_glog_end:],
        w_in[:, :, c_q_end:c_kv_end],
        w_in[:, :, c_kv_end:c_glog_end],
        jnp.zeros((n_layers, d, LANES - N_GLOG), w_in.dtype),
    ], axis=-1).astype(BF16)


def kernel(x, mem, pre_norm_g, post_norm_g, mem_norm_g, w_in, b_gate, conv_w, conv_b, cmp_pos_k, cmp_w1_k, cmp_w2_k, cmp_pos_v, cmp_w1_v, cmp_w2_v, w_mem_kv, w_out):
    b, t, d = x.shape
    n_layers = w_in.shape[0]
    assert t % ROW_TILE == 0 and t % SEL_CHUNK == 0 and t >= WINDOW + Q_BLOCK
    assert t // SEL_BLOCK <= MAX_SEL_BLOCKS
    nc = t // CMP_STRIDE

    w_in_p = _permute_w_in(w_in)
    w_out_b = w_out.astype(BF16)
    memkv = _memkv(mem, mem_norm_g[:, None, :], w_mem_kv.astype(BF16))
    w1 = jnp.stack([cmp_w1_k, cmp_w1_v], axis=1).astype(BF16)
    w2 = jnp.stack([cmp_w2_k, cmp_w2_v], axis=1).astype(BF16)
    pos = jnp.stack([cmp_pos_k, cmp_pos_v], axis=1).reshape(n_layers, 2, 1, CMP_BLOCK * HEAD_DIM)
    bgate = jnp.pad(b_gate, ((0, 0), (0, LANES - N_GLOG)))[:, None, :]

    key_blk = jnp.arange(t, dtype=jnp.int32)[:, None] // SEL_BLOCK
    et = (key_blk == jnp.arange(LANES, dtype=jnp.int32)[None, :]).astype(BF16)
    j = jnp.arange(nc, dtype=jnp.int32)[:, None]
    blk = jnp.arange(LANES, dtype=jnp.int32)[None, :]
    per_blk = SEL_BLOCK // CMP_STRIDE
    mimp = ((j // per_blk == blk).astype(F32) + ((j + 1) // per_blk == blk).astype(F32)) * (j < nc - 1)

    for l in range(n_layers):
        proj = _inproj(x.reshape(b * t, d), pre_norm_g[l][None, :], w_in_p[l]).reshape(b, t, PROJ_WIDTH)
        kv_cmp = proj[:, :, OFF_KC:OFF_KC + 2 * KV_WIDTH].reshape(b, nc, CMP_STRIDE, 2, NSA_GROUPS, HEAD_DIM)
        kv_cmp = kv_cmp.transpose(3, 0, 4, 1, 2, 5).reshape(2, b, NSA_GROUPS, nc, CMP_STRIDE * HEAD_DIM)
        kvc = _compress(kv_cmp, w1[l], pos[l], w2[l])
        ynsa = _nsa(proj, kvc, et, mimp, bgate[l])
        x = _mix_out(proj, ynsa, memkv[l], conv_w[l], conv_b[l][None, :], w_out_b[l], post_norm_g[l][None, :], x)
    return x
```
